```python
import jax, jax.numpy as jnp
from jax import lax
import numpy as np

D_MODEL = 2048
BATCH = 2
SEQ = 8192
DEPTH = 1

D_MIX = D_MODEL
LRU_WIDTH = D_MIX // 2
CONF_WIDTH = D_MIX - LRU_WIDTH
LRU_HEADS = 8
LRU_HEAD_DIM = LRU_WIDTH // LRU_HEADS
LRU_CONV_WIDTH = 4
LRU_C = 8.0
CONF_GROUPS = 8
CONF_KERNEL = 31
IN_COLS = 2 * LRU_WIDTH + 2 * CONF_WIDTH
PEER_HEADS = 8
PEER_N_KEYS = 128
PEER_N_EXPERTS = PEER_N_KEYS * PEER_N_KEYS
PEER_TOPK = 16
PEER_QUERY_DIM = 256
PEER_HALF = PEER_QUERY_DIM // 2
PEER_BLOCK = 128
N_MOD = 6
EPS = 1e-6

kernel_name = "hymba_rglru_conformer_peer_block"


def rms_norm(x, g):
    xf = x.astype(jnp.float32)
    y = xf * lax.rsqrt(jnp.mean(xf * xf, axis=-1, keepdims=True) + EPS)
    return (y * g.astype(jnp.float32)).astype(x.dtype)


def layer_norm(x, g, b):
    xf = x.astype(jnp.float32)
    mu = jnp.mean(xf, axis=-1, keepdims=True)
    var = jnp.mean(jnp.square(xf - mu), axis=-1, keepdims=True)
    y = (xf - mu) * lax.rsqrt(var + EPS)
    return (y * g.astype(jnp.float32) + b.astype(jnp.float32)).astype(x.dtype)


def modulate(h, shift, scale):
    return h * (1.0 + scale[:, None, :]) + shift[:, None, :]


def causal_depthwise_conv(x, w, b):
    k = w.shape[0]
    y = lax.conv_general_dilated(
        x, w[:, None, :].astype(x.dtype), window_strides=(1,), padding=[(k - 1, 0)],
        dimension_numbers=('NWC', 'WIO', 'NWC'), feature_group_count=x.shape[-1])
    return y + b


def _linear_recurrence_combine(e1, e2):
    a1, b1 = e1
    a2, b2 = e2
    return a1 * a2, a2 * b1 + b2


def rg_lru(x, wa, ba, wx, bx, lam):
    bsz, seq, width = x.shape
    xh = x.reshape(bsz, seq, LRU_HEADS, LRU_HEAD_DIM)
    r = jax.nn.sigmoid(jnp.einsum('bshi,hij->bshj', xh, wa).reshape(bsz, seq, width) + ba)
    i = jax.nn.sigmoid(jnp.einsum('bshi,hij->bshj', xh, wx).reshape(bsz, seq, width) + bx)
    log_a = -LRU_C * jax.nn.softplus(-lam.astype(jnp.float32)) * r.astype(jnp.float32)
    a = jnp.exp(log_a)
    mult = jnp.sqrt(-jnp.expm1(2.0 * log_a))
    u = mult * (i * x).astype(jnp.float32)
    _, h = lax.associative_scan(_linear_recurrence_combine, (a, u), axis=1)
    return h.astype(x.dtype)


def hybrid_mixer(h, w_in, lru_conv_w, lru_conv_b, lru_wa, lru_ba, lru_wx, lru_bx, lru_lambda,
                 conf_dw_w, conf_dw_b, conf_ln_g, conf_ln_b, w_out):
    proj = h @ w_in
    lru_x, lru_g, conf_v, conf_g = jnp.split(
        proj, [LRU_WIDTH, 2 * LRU_WIDTH, 2 * LRU_WIDTH + CONF_WIDTH], axis=-1)
    xl = causal_depthwise_conv(lru_x, lru_conv_w, lru_conv_b)
    y_lru = rg_lru(xl, lru_wa, lru_ba, lru_wx, lru_bx, lru_lambda) * jax.nn.gelu(lru_g)
    z = conf_v * jax.nn.sigmoid(conf_g)
    z = causal_depthwise_conv(z, conf_dw_w, conf_dw_b)
    z = jax.nn.silu(layer_norm(z, conf_ln_g, conf_ln_b))
    return jnp.concatenate([y_lru, z], axis=-1) @ w_out


def peer_ffn(h, wq, subkeys, u_table, v_table):
    bsz, seq, d = h.shape
    t = bsz * seq
    hf = h.reshape(t, d)
    q = (hf @ wq).reshape(t, PEER_HEADS, 2, PEER_HALF)
    scores = jnp.einsum('thpd,hpnd->thpn', q.astype(jnp.float32), subkeys.astype(jnp.float32))
    s1, i1 = lax.top_k(scores[:, :, 0], PEER_TOPK)
    s2, i2 = lax.top_k(scores[:, :, 1], PEER_TOPK)
    cand = (s1[..., :, None] + s2[..., None, :]).reshape(t, PEER_HEADS, PEER_TOPK * PEER_TOPK)
    cs, ci = lax.top_k(cand, PEER_TOPK)
    row = jnp.take_along_axis(i1, ci // PEER_TOPK, axis=-1)
    col = jnp.take_along_axis(i2, ci % PEER_TOPK, axis=-1)
    expert_idx = row * PEER_N_KEYS + col
    gates = jax.nn.softmax(cs, axis=-1).astype(h.dtype)
    n_blk = t // PEER_BLOCK

    def expert_block(args):
        hb, ib, gb = args
        u = jnp.take(u_table, ib, axis=0)
        act = jnp.einsum('cd,chkd->chk', hb, u)
        w = gb * jax.nn.gelu(act)
        v = jnp.take(v_table, ib, axis=0)
        return jnp.einsum('chk,chkd->cd', w, v)

    out = lax.map(expert_block, (hf.reshape(n_blk, PEER_BLOCK, d),
                                 expert_idx.reshape(n_blk, PEER_BLOCK, PEER_HEADS, PEER_TOPK),
                                 gates.reshape(n_blk, PEER_BLOCK, PEER_HEADS, PEER_TOPK)))
    return out.reshape(bsz, seq, d)


def setup_inputs(seed: int = 0) -> dict:
    key = jax.random.key(seed)
    ks = jax.random.split(key, 26)
    f32 = jnp.float32
    nrm = lambda k, shape, s: jax.random.normal(k, shape, f32) * s
    gain = lambda k, n: 1.0 + 0.02 * jax.random.normal(k, (DEPTH, n), f32)
    u_a = jax.random.uniform(ks[14], (DEPTH, LRU_WIDTH), f32, 0.9, 0.999)
    s_l = u_a ** (1.0 / LRU_C)
    lru_lambda = jnp.log(s_l) - jnp.log1p(-s_l)
    return {
        "x": jax.random.normal(ks[0], (BATCH, SEQ, D_MODEL), f32),
        "c": jax.random.normal(ks[1], (BATCH, D_MODEL), f32),
        "w_mod": nrm(ks[2], (DEPTH, D_MODEL, N_MOD * D_MODEL), 0.2 * D_MODEL ** -0.5),
        "b_mod": nrm(ks[3], (DEPTH, N_MOD * D_MODEL), 0.02),
        "g_pre_mix": gain(ks[4], D_MODEL),
        "g_post_mix": gain(ks[5], D_MODEL),
        "g_pre_ffn": gain(ks[6], D_MODEL),
        "g_post_ffn": gain(ks[7], D_MODEL),
        "w_in": nrm(ks[8], (DEPTH, D_MODEL, IN_COLS), D_MODEL ** -0.5),
        "lru_conv_w": nrm(ks[9], (DEPTH, LRU_CONV_WIDTH, LRU_WIDTH), LRU_CONV_WIDTH ** -0.5),
        "lru_conv_b": nrm(ks[10], (DEPTH, LRU_WIDTH), 0.02),
        "lru_wa": nrm(ks[11], (DEPTH, LRU_HEADS, LRU_HEAD_DIM, LRU_HEAD_DIM), LRU_HEAD_DIM ** -0.5),
        "lru_ba": nrm(ks[12], (DEPTH, LRU_WIDTH), 0.02),
        "lru_wx": nrm(ks[13], (DEPTH, LRU_HEADS, LRU_HEAD_DIM, LRU_HEAD_DIM), LRU_HEAD_DIM ** -0.5),
        "lru_bx": nrm(ks[15], (DEPTH, LRU_WIDTH), 0.02),
        "lru_lambda": lru_lambda,
        "conf_dw_w": nrm(ks[16], (DEPTH, CONF_KERNEL, CONF_WIDTH), CONF_KERNEL ** -0.5),
        "conf_dw_b": nrm(ks[17], (DEPTH, CONF_WIDTH), 0.02),
        "conf_ln_g": gain(ks[18], CONF_WIDTH),
        "conf_ln_b": nrm(ks[19], (DEPTH, CONF_WIDTH), 0.02),
        "w_out": nrm(ks[20], (DEPTH, D_MIX, D_MODEL), D_MIX ** -0.5),
        "peer_wq": nrm(ks[21], (DEPTH, D_MODEL, PEER_HEADS * PEER_QUERY_DIM), D_MODEL ** -0.5),
        "peer_subkeys": nrm(ks[22], (DEPTH, PEER_HEADS, 2, PEER_N_KEYS, PEER_HALF), PEER_HALF ** -0.5),
        "peer_u": nrm(ks[23], (DEPTH, PEER_N_EXPERTS, D_MODEL), D_MODEL ** -0.5),
        "peer_v": nrm(ks[24], (DEPTH, PEER_N_EXPERTS, D_MODEL), PEER_HEADS ** -0.5),
    }


def reference(x, c, w_mod, b_mod, g_pre_mix, g_post_mix, g_pre_ffn, g_post_ffn, w_in,
              lru_conv_w, lru_conv_b, lru_wa, lru_ba, lru_wx, lru_bx, lru_lambda,
              conf_dw_w, conf_dw_b, conf_ln_g, conf_ln_b, w_out,
              peer_wq, peer_subkeys, peer_u, peer_v):
    c_act = jax.nn.silu(c)
    for l in range(DEPTH):
        mod = c_act @ w_mod[l] + b_mod[l]
        sh1, sc1, gt1, sh2, sc2, gt2 = jnp.split(mod, N_MOD, axis=-1)
        h = modulate(rms_norm(x, g_pre_mix[l]), sh1, sc1)
        y = hybrid_mixer(h, w_in[l], lru_conv_w[l], lru_conv_b[l], lru_wa[l], lru_ba[l],
                         lru_wx[l], lru_bx[l], lru_lambda[l], conf_dw_w[l], conf_dw_b[l],
                         conf_ln_g[l], conf_ln_b[l], w_out[l])
        x = x + gt1[:, None, :] * rms_norm(y, g_post_mix[l])
        h = modulate(rms_norm(x, g_pre_ffn[l]), sh2, sc2)
        y = peer_ffn(h, peer_wq[l], peer_subkeys[l], peer_u[l], peer_v[l])
        x = x + gt2[:, None, :] * rms_norm(y, g_post_ffn[l])
    return x
```

```python
import functools

import jax
import jax.numpy as jnp
from jax import lax
from jax.experimental import pallas as pl
from jax.experimental.pallas import tpu as pltpu

EPS = 1e-6
LRU_C = 8.0
LRU_HEADS = 8
LRU_CONV_WIDTH = 4
CONF_KERNEL = 31
PEER_HEADS = 8
PEER_N_KEYS = 128
PEER_TOPK = 16
PEER_HALF = 128
N_MOD = 6

LANES = 128
SUBLANES = 8
VMEM_LIMIT = 56 * 1024 * 1024

F32 = jnp.float32
BF16 = jnp.bfloat16
NEG_INF = float("-inf")


def _cparams(sem):
    return pltpu.CompilerParams(dimension_semantics=sem, vmem_limit_bytes=VMEM_LIMIT)


def _const_spec(shape):
    nd = len(shape)
    return pl.BlockSpec(shape, lambda *_: (0,) * nd, pipeline_mode=pl.Buffered(1))


def _rms(x, g):
    return x * lax.rsqrt(jnp.mean(x * x, axis=-1, keepdims=True) + EPS) * g


def _gelu_tanh(x):
    c = 0.7978845608028654
    return 0.5 * x * (1.0 + jnp.tanh(c * (x + 0.044715 * (x * x * x))))


def _sigmoid(x):
    return 1.0 / (1.0 + jnp.exp(-x))


def _mod_kernel(c_ref, w_ref, b_ref, o_ref):
    c = c_ref[...]
    ca = c * _sigmoid(c)
    o_ref[...] = jnp.dot(ca, w_ref[...], preferred_element_type=F32,
                         precision=lax.Precision.HIGHEST) + b_ref[...]


def _modulation(c, w_mod, b_mod):
    bsz, d = c.shape
    n = w_mod.shape[1]
    bn = 1024
    cp = jnp.zeros((SUBLANES, d), F32).at[:bsz].set(c)
    out = pl.pallas_call(
        _mod_kernel,
        grid=(n // bn,),
        in_specs=[_const_spec((SUBLANES, d)),
                  pl.BlockSpec((d, bn), lambda j: (0, j)),
                  pl.BlockSpec((1, bn), lambda j: (0, j))],
        out_specs=pl.BlockSpec((SUBLANES, bn), lambda j: (0, j)),
        out_shape=jax.ShapeDtypeStruct((SUBLANES, n), F32),
        compiler_params=_cparams(("parallel",)),
        name="mod",
    )(cp, w_mod, b_mod.reshape(1, n))
    return out[:bsz]


def _inproj_kernel(x_ref, g_ref, sh_ref, sc_ref, w_ref, o_ref):
    h = _rms(x_ref[...], g_ref[...]) * (1.0 + sc_ref[...]) + sh_ref[...]
    o_ref[...] = jnp.dot(h.astype(BF16), w_ref[...], preferred_element_type=F32)


def _inproj(x2, g, sh, sc, w_bf, tiles_per_batch, tm):
    t, d = x2.shape
    n = w_bf.shape[1]
    return pl.pallas_call(
        _inproj_kernel,
        grid=(t // tm,),
        in_specs=[pl.BlockSpec((tm, d), lambda i: (i, 0)),
                  _const_spec((1, d)),
                  pl.BlockSpec((None, 1, d), lambda i: (i // tiles_per_batch, 0, 0)),
                  pl.BlockSpec((None, 1, d), lambda i: (i // tiles_per_batch, 0, 0)),
                  _const_spec((d, n))],
        out_specs=pl.BlockSpec((tm, n), lambda i: (i, 0)),
        out_shape=jax.ShapeDtypeStruct((t, n), F32),
        compiler_params=_cparams(("parallel",)),
        name="inproj",
    )(x2, g, sh, sc, w_bf)


LRU_HIST = 8
CONF_HIST = 32


def _shift_rows(x, d, rows, fill):
    return jnp.where(rows >= d, pltpu.roll(x, d, axis=0), fill)


def _mixer_kernel(p_ref, cw_ref, cb_ref, wa_ref, ba_ref, wx_ref, bx_ref, lam_ref,
                  dw_ref, db_ref, lg_ref, lb_ref, o_ref,
                  xext_ref, zext_ref, carry_ref, *, ts, w):
    s = pl.program_id(1)

    @pl.when(s == 0)
    def _():
        xext_ref[0:LRU_HIST, :] = jnp.zeros((LRU_HIST, w), F32)
        zext_ref[0:CONF_HIST, :] = jnp.zeros((CONF_HIST, w), F32)
        carry_ref[...] = jnp.zeros_like(carry_ref)

    xext_ref[LRU_HIST:LRU_HIST + ts, :] = p_ref[:, 0:w]
    xl = jnp.zeros((ts, w), F32) + cb_ref[...]
    for j in range(LRU_CONV_WIDTH):
        off = LRU_HIST - (LRU_CONV_WIDTH - 1) + j
        xl = xl + cw_ref[j:j + 1, :] * xext_ref[pl.ds(off, ts), :]
    xext_ref[0:LRU_HIST, :] = xext_ref[ts:ts + LRU_HIST, :]

    xb = xl.astype(BF16)
    hd = w // LRU_HEADS
    r_parts, i_parts = [], []
    for hh in range(LRU_HEADS):
        xh = xb[:, hh * hd:(hh + 1) * hd]
        r_parts.append(jnp.dot(xh, wa_ref[hh], preferred_element_type=F32))
        i_parts.append(jnp.dot(xh, wx_ref[hh], preferred_element_type=F32))
    r = _sigmoid(jnp.concatenate(r_parts, axis=-1) + ba_ref[...])
    ig = _sigmoid(jnp.concatenate(i_parts, axis=-1) + bx_ref[...])

    nl = -lam_ref[...]
    softplus = jnp.maximum(nl, 0.0) + jnp.log1p(jnp.exp(-jnp.abs(nl)))
    log_a = (-LRU_C * softplus) * r
    a = jnp.exp(log_a)
    th = jnp.tanh(-log_a)
    mult = jnp.sqrt(2.0 * th / (1.0 + th))
    b = mult * (ig * xl)

    rows = lax.broadcasted_iota(jnp.int32, (ts, w), 0)
    d = 1
    while d < ts:
        a_sh = _shift_rows(a, d, rows, 1.0)
        b_sh = _shift_rows(b, d, rows, 0.0)
        b = a * b_sh + b
        a = a * a_sh
        d *= 2
    h = b + a * carry_ref[0:1, :]
    carry_ref[...] = jnp.broadcast_to(h[ts - 1:ts, :], carry_ref.shape)
    o_ref[:, 0:w] = (h * _gelu_tanh(p_ref[:, w:2 * w])).astype(o_ref.dtype)

    z = p_ref[:, 2 * w:3 * w] * _sigmoid(p_ref[:, 3 * w:4 * w])
    zext_ref[CONF_HIST:CONF_HIST + ts, :] = z
    y = jnp.zeros((ts, w), F32) + db_ref[...]
    for j in range(CONF_KERNEL):
        off = CONF_HIST - (CONF_KERNEL - 1) + j
        y = y + dw_ref[j:j + 1, :] * zext_ref[pl.ds(off, ts), :]
    zext_ref[0:CONF_HIST, :] = zext_ref[ts:ts + CONF_HIST, :]
    mu = jnp.mean(y, axis=-1, keepdims=True)
    yc = y - mu
    var = jnp.mean(yc * yc, axis=-1, keepdims=True)
    yn = yc * lax.rsqrt(var + EPS) * lg_ref[...] + lb_ref[...]
    o_ref[:, w:2 * w] = (yn * _sigmoid(yn)).astype(o_ref.dtype)


def _mixer(proj3, cw, cb, wa_bf, ba, wx_bf, bx, lam, dw, db, lg, lb, ts):
    bsz, seq, n = proj3.shape
    w = n // 4
    hd = w // LRU_HEADS
    row = lambda a: a.reshape(1, w)
    kern = functools.partial(_mixer_kernel, ts=ts, w=w)
    return pl.pallas_call(
        kern,
        grid=(bsz, seq // ts),
        in_specs=[pl.BlockSpec((None, ts, n), lambda b, s: (b, s, 0)),
                  _const_spec((LRU_CONV_WIDTH, w)), _const_spec((1, w)),
                  _const_spec((LRU_HEADS, hd, hd)), _const_spec((1, w)),
                  _const_spec((LRU_HEADS, hd, hd)), _const_spec((1, w)),
                  _const_spec((1, w)),
                  _const_spec((CONF_KERNEL, w)), _const_spec((1, w)),
                  _const_spec((1, w)), _const_spec((1, w))],
        out_specs=pl.BlockSpec((None, ts, 2 * w), lambda b, s: (b, s, 0)),
        out_shape=jax.ShapeDtypeStruct((bsz, seq, 2 * w), BF16),
        scratch_shapes=[pltpu.VMEM((LRU_HIST + ts, w), F32),
                        pltpu.VMEM((CONF_HIST + ts, w), F32),
                        pltpu.VMEM((SUBLANES, w), F32)],
        compiler_params=_cparams(("arbitrary", "arbitrary")),
        name="mixer",
    )(proj3, cw, row(cb), wa_bf, row(ba), wx_bf, row(bx), row(lam), dw, row(db), row(lg), row(lb))


def _outproj_kernel(y_ref, x_ref, w_ref, gpost_ref, gt_ref, gpre_ref, sh_ref, sc_ref,
                    x1_ref, h2t_ref):
    y = jnp.dot(y_ref[...], w_ref[...], preferred_element_type=F32)
    x1 = x_ref[...] + gt_ref[...] * _rms(y, gpost_ref[...])
    x1_ref[...] = x1
    h2 = _rms(x1, gpre_ref[...]) * (1.0 + sc_ref[...]) + sh_ref[...]
    h2t_ref[...] = h2.T.astype(h2t_ref.dtype)


def _outproj(ycat, x2, w_bf, gpost, gt1, gpre, sh2, sc2, tiles_per_batch, tm):
    t, d = x2.shape
    per_batch = pl.BlockSpec((None, 1, d), lambda i: (i // tiles_per_batch, 0, 0))
    return pl.pallas_call(
        _outproj_kernel,
        grid=(t // tm,),
        in_specs=[pl.BlockSpec((tm, d), lambda i: (i, 0)),
                  pl.BlockSpec((tm, d), lambda i: (i, 0)),
                  _const_spec((d, d)), _const_spec((1, d)), per_batch,
                  _const_spec((1, d)), per_batch, per_batch],
        out_specs=[pl.BlockSpec((tm, d), lambda i: (i, 0)),
                   pl.BlockSpec((d, tm), lambda i: (0, i))],
        out_shape=[jax.ShapeDtypeStruct((t, d), F32),
                   jax.ShapeDtypeStruct((d, t), BF16)],
        compiler_params=_cparams(("parallel",)),
        name="outproj",
    )(ycat, x2, w_bf, gpost, gt1, gpre, sh2, sc2)


def _top_values(s, k, rows):
    nrow = s.shape[0]
    vals = []
    for _ in range(k):
        m = jnp.max(s, axis=0, keepdims=True)
        vals.append(m)
        first = jnp.min(jnp.where(s == m, rows, nrow), axis=0, keepdims=True)
        s = jnp.where(rows == first, NEG_INF, s)
    return vals


def _peerkeys_kernel(ht_ref, wq_ref, keys_ref, s1_ref, a_ref, s2_ref, e2_ref, tau_ref,
                     *, tb):
    qt = jnp.dot(wq_ref[...], ht_ref[...], preferred_element_type=F32)
    rows = lax.broadcasted_iota(jnp.int32, (PEER_N_KEYS, tb), 0)
    taus = []
    for hh in range(PEER_HEADS):
        sc = []
        for p in range(2):
            hp = 2 * hh + p
            qh = qt[hp * PEER_HALF:(hp + 1) * PEER_HALF, :].astype(BF16)
            sc.append(jnp.dot(keys_ref[hp], qh, preferred_element_type=F32))
        s1, s2 = sc
        c1 = _top_values(s1, PEER_TOPK, rows)
        c2 = _top_values(s2, PEER_TOPK, rows)
        cand = [c1[ia] + c2[ib] for ia in range(PEER_TOPK)
                for ib in range(PEER_TOPK // (ia + 1))]
        npad = -len(cand) % SUBLANES
        cand = jnp.concatenate(cand + [jnp.full((npad, tb), NEG_INF, F32)], axis=0)
        crow = lax.broadcasted_iota(jnp.int32, cand.shape, 0)
        tau = _top_values(cand, PEER_TOPK, crow)[-1]
        cmax = c1[0] + c2[0]
        z = jnp.sum(jnp.where(cand >= tau, jnp.exp(cand - cmax), 0.0), axis=0, keepdims=True)
        s1_ref[hh] = s1
        s2_ref[hh] = s2
        a_ref[hh] = jnp.exp(s1 - c1[0]) / z
        e2_ref[hh] = jnp.exp(s2 - c2[0])
        taus.append(tau)
    tau_ref[...] = jnp.concatenate(taus, axis=0)


def _peerkeys(h2t, wqt_bf, keys_bf, tb):
    d, t = h2t.shape
    nq = wqt_bf.shape[0]
    sk = pl.BlockSpec((PEER_HEADS, PEER_N_KEYS, tb), lambda i: (0, 0, i))
    sshape = jax.ShapeDtypeStruct((PEER_HEADS, PEER_N_KEYS, t), F32)
    return pl.pallas_call(
        functools.partial(_peerkeys_kernel, tb=tb),
        grid=(t // tb,),
        in_specs=[pl.BlockSpec((d, tb), lambda i: (0, i)),
                  _const_spec((nq, d)),
                  _const_spec((2 * PEER_HEADS, PEER_N_KEYS, PEER_HALF))],
        out_specs=[sk, sk, sk, sk, pl.BlockSpec((PEER_HEADS, tb), lambda i: (0, i))],
        out_shape=[sshape, sshape, sshape, sshape,
                   jax.ShapeDtypeStruct((PEER_HEADS, t), F32)],
        compiler_params=_cparams(("parallel",)),
        name="peerkeys",
    )(h2t, wqt_bf, keys_bf)


def _peerffn_kernel(ht_ref, u_ref, vt_ref, s1_ref, a_ref, s2_ref, e2_ref, tau_ref,
                    x1_ref, gt_ref, g_ref, o_ref, acc_ref, wt_ref, *, tb, eb, tc):
    j = pl.program_id(1)
    nrow = eb // PEER_N_KEYS

    @pl.when(j == 0)
    def _():
        acc_ref[...] = jnp.zeros_like(acc_ref)

    st = jnp.dot(u_ref[...], ht_ref[...], preferred_element_type=F32)
    for rr in range(nrow):
        r = j * nrow + rr
        for c0 in range(0, tb, tc):
            cs = pl.ds(c0, tc)
            gate = jnp.zeros((PEER_N_KEYS, tc), F32)
            for hh in range(PEER_HEADS):
                cand = s1_ref[hh, pl.ds(r, 1), cs] + s2_ref[hh, :, cs]
                val = e2_ref[hh, :, cs] * a_ref[hh, pl.ds(r, 1), cs]
                gate = gate + jnp.where(cand >= tau_ref[hh:hh + 1, cs], val, 0.0)
            act = st[rr * PEER_N_KEYS:(rr + 1) * PEER_N_KEYS, c0:c0 + tc]
            wt_ref[rr * PEER_N_KEYS:(rr + 1) * PEER_N_KEYS, cs] = (
                gate * _gelu_tanh(act)).astype(wt_ref.dtype)
    acc_ref[...] += jnp.dot(vt_ref[...], wt_ref[...], preferred_element_type=F32)

    @pl.when(j == pl.num_programs(1) - 1)
    def _():
        y = acc_ref[...].T
        o_ref[...] = x1_ref[...] + gt_ref[...] * _rms(y, g_ref[...])


def _peerffn(h2t, u_bf, vt_bf, s1, a, s2, e2, tau, x1, gt2, gpost, tiles_per_batch, tb, eb):
    d, t = h2t.shape
    n = u_bf.shape[0]
    once = dict(pipeline_mode=pl.Buffered(1))
    tok = pl.BlockSpec((PEER_HEADS, PEER_N_KEYS, tb), lambda i, j: (0, 0, i), **once)
    kern = functools.partial(_peerffn_kernel, tb=tb, eb=eb, tc=2 * LANES)
    return pl.pallas_call(
        kern,
        grid=(t // tb, n // eb),
        in_specs=[pl.BlockSpec((d, tb), lambda i, j: (0, i)),
                  pl.BlockSpec((eb, d), lambda i, j: (j, 0)),
                  pl.BlockSpec((d, eb), lambda i, j: (0, j)),
                  tok, tok, tok, tok,
                  pl.BlockSpec((PEER_HEADS, tb), lambda i, j: (0, i)),
                  pl.BlockSpec((tb, d), lambda i, j: (i, 0), **once),
                  pl.BlockSpec((None, 1, d), lambda i, j: (i // tiles_per_batch, 0, 0)),
                  _const_spec((1, d))],
        out_specs=pl.BlockSpec((tb, d), lambda i, j: (i, 0)),
        out_shape=jax.ShapeDtypeStruct((t, d), F32),
        scratch_shapes=[pltpu.VMEM((d, tb), F32), pltpu.VMEM((eb, tb), BF16)],
        compiler_params=_cparams(("parallel", "arbitrary")),
        name="peerffn",
    )(h2t, u_bf, vt_bf, s1, a, s2, e2, tau, x1, gt2, gpost)


def _layer(x, c, w_mod, b_mod, g_pre_mix, g_post_mix, g_pre_ffn, g_post_ffn, w_in,
           lru_conv_w, lru_conv_b, lru_wa, lru_ba, lru_wx, lru_bx, lru_lambda,
           conf_dw_w, conf_dw_b, conf_ln_g, conf_ln_b, w_out,
           peer_wq, peer_subkeys, peer_u, peer_v):
    bsz, seq, d = x.shape
    t = bsz * seq
    tm = min(256, seq)
    ts = min(256, seq)
    tb_keys = min(256, seq)
    tb_ffn = min(512, seq)
    eb = 1024

    mod = _modulation(c, w_mod, b_mod)
    sh1, sc1, gt1, sh2, sc2, gt2 = [m.reshape(bsz, 1, d) for m in jnp.split(mod, N_MOD, axis=-1)]
    row = lambda g: g.reshape(1, d)

    x2 = x.reshape(t, d)
    proj = _inproj(x2, row(g_pre_mix), sh1, sc1, w_in.astype(BF16), seq // tm, tm)
    ycat = _mixer(proj.reshape(bsz, seq, -1), lru_conv_w, lru_conv_b,
                  lru_wa.astype(BF16), lru_ba, lru_wx.astype(BF16), lru_bx, lru_lambda,
                  conf_dw_w, conf_dw_b, conf_ln_g, conf_ln_b, ts)
    x1, h2t = _outproj(ycat.reshape(t, d), x2, w_out.astype(BF16), row(g_post_mix), gt1,
                       row(g_pre_ffn), sh2, sc2, seq // tm, tm)

    keys_bf = peer_subkeys.reshape(2 * PEER_HEADS, PEER_N_KEYS, PEER_HALF).astype(BF16)
    s1, a, s2, e2, tau = _peerkeys(h2t, peer_wq.T.astype(BF16), keys_bf, tb_keys)
    out = _peerffn(h2t, peer_u.astype(BF16), peer_v.T.astype(BF16), s1, a, s2, e2, tau,
                   x1, gt2, row(g_post_ffn), seq // tb_ffn, tb_ffn, eb)
    return out.reshape(bsz, seq, d)


def kernel(x, c, w_mod, b_mod, g_pre_mix, g_post_mix, g_pre_ffn, g_post_ffn, w_in, lru_conv_w, lru_conv_b, lru_wa, lru_ba, lru_wx, lru_bx, lru_lambda, conf_dw_w, conf_dw_b, conf_ln_g, conf_ln_b, w_out, peer_wq, peer_subkeys, peer_u, peer_v):
    depth = w_mod.shape[0]
    for l in range(depth):
        x = _layer(x, c, w_mod[l], b_mod[l], g_pre_mix[l], g_post_mix[l], g_pre_ffn[l],
                   g_post_ffn[l], w_in[l], lru_conv_w[l], lru_conv_b[l], lru_wa[l], lru_ba[l],
                   lru_wx[l], lru_bx[l], lru_lambda[l], conf_dw_w[l], conf_dw_b[l],
                   conf_ln_g[l], conf_ln_b[l], w_out[l], peer_wq[l], peer_subkeys[l],
                   peer_u[l], peer_v[l])
    return x
```
